```python
import jax, jax.numpy as jnp
from jax import lax
import numpy as np

D_MODEL = 1024
BATCH = 4
SEQ = 4096
DEPTH = 4

CHUNK = 64
D_CONV = D_MODEL
CONV_WIDTH = 31
D_SGU = D_MODEL
SGU_BLOCK = 128
SGU_GROUPS = 8
SGU_GROUP_DIM = D_SGU // SGU_GROUPS
D_IN = 2 * D_CONV + 2 * D_SGU + 2 * D_MODEL
SPLITS = (D_CONV, 2 * D_CONV, 2 * D_CONV + 2 * D_SGU, 2 * D_CONV + 2 * D_SGU + D_MODEL)
N_EXPERTS = 32
TOP_K = 4
D_EXPERT = D_MODEL
SWIGLU_LIMIT = 7.0
SWIGLU_ALPHA = 1.702
MOE_BLOCK = 128
LN_EPS = 1e-5
DEEPNORM_ALPHA = float((2 * DEPTH) ** 0.25)
DEEPNORM_BETA = float((8 * DEPTH) ** -0.25)

kernel_name = "hybrid_conv_sgu_moe_deepnorm_adaln"


def _layer_norm(x, gain=None, bias=None):
    xf = x.astype(jnp.float32)
    mu = jnp.mean(xf, axis=-1, keepdims=True)
    var = jnp.mean(jnp.square(xf - mu), axis=-1, keepdims=True)
    y = (xf - mu) * lax.rsqrt(var + LN_EPS)
    if gain is not None:
        y = y * gain.astype(jnp.float32) + bias.astype(jnp.float32)
    return y.astype(x.dtype)


def _causal_depthwise_conv(x, w, b):
    y = lax.conv_general_dilated(
        x, w[:, None, :], window_strides=(1,), padding=[(CONV_WIDTH - 1, 0)],
        dimension_numbers=('NWC', 'WIO', 'NWC'), feature_group_count=x.shape[-1])
    return y + b


def _spatial_gating(u, v, ln_g, ln_b, w_s, b_s):
    B, S, _ = v.shape
    n_blk = S // SGU_BLOCK
    v = _layer_norm(v, ln_g, ln_b).reshape(B, n_blk, SGU_BLOCK, SGU_GROUPS, SGU_GROUP_DIM)
    chunk_id = jnp.arange(SGU_BLOCK) // CHUNK
    mask = chunk_id[:, None] >= chunk_id[None, :]
    w = jnp.where(mask[None], w_s, 0)
    mixed = jnp.einsum('gij,bnjgc->bnigc', w, v) + jnp.transpose(b_s)[None, None, :, :, None]
    return u * mixed.reshape(B, S, D_SGU)


def _token_mixer(h, w_in, b_in, conv_w, conv_b, ln_a_g, ln_a_b, w_a, b_a,
                 ln_v_g, ln_v_b, w_s, b_s, w_b, b_b, w_out, b_out):
    z = h @ w_in + b_in
    a_val, a_gate, sgu, g_a, g_b = jnp.split(z, SPLITS, axis=-1)
    ya = a_val * jax.nn.sigmoid(a_gate)
    ya = _causal_depthwise_conv(ya, conv_w, conv_b)
    ya = jax.nn.silu(_layer_norm(ya, ln_a_g, ln_a_b))
    ya = ya @ w_a + b_a
    u, v = jnp.split(jax.nn.gelu(sgu, approximate=False), 2, axis=-1)
    yb = _spatial_gating(u, v, ln_v_g, ln_v_b, w_s, b_s) @ w_b + b_b
    m = jax.nn.sigmoid(g_a) * ya + jax.nn.sigmoid(g_b) * yb
    return m @ w_out + b_out


def _moe(h, w_router, b_router, w_gu, b_gu, w_dn, b_dn):
    B, S, D = h.shape
    T = B * S
    xt = h.reshape(T, D)
    logits = (xt @ w_router + b_router).astype(jnp.float32)
    top_logits, top_idx = lax.top_k(logits, TOP_K)
    top_w = jax.nn.softmax(top_logits, axis=-1)
    flat_e = top_idx.reshape(-1)
    order = jnp.argsort(flat_e)
    sorted_e = flat_e[order]
    counts = jnp.bincount(flat_e, length=N_EXPERTS)
    padded = (counts + MOE_BLOCK - 1) // MOE_BLOCK * MOE_BLOCK
    start = jnp.cumsum(counts) - counts
    pad_end = jnp.cumsum(padded)
    pad_start = pad_end - padded
    dest = pad_start[sorted_e] + jnp.arange(T * TOP_K) - start[sorted_e]
    n_slots = T * TOP_K + N_EXPERTS * MOE_BLOCK
    n_blocks = n_slots // MOE_BLOCK
    slot_tok = jnp.full((n_slots,), T, jnp.int32).at[dest].set((order // TOP_K).astype(jnp.int32))
    slot_w = jnp.zeros((n_slots,), jnp.float32).at[dest].set(top_w.reshape(-1)[order])
    block_e = jnp.minimum(
        jnp.searchsorted(pad_end, jnp.arange(n_blocks) * MOE_BLOCK, side='right'), N_EXPERTS - 1)
    xt_pad = jnp.concatenate([xt, jnp.zeros((1, D), xt.dtype)], axis=0)

    def expert_block(args):
        tok, e = args
        xb = xt_pad[tok]
        gate, up = jnp.split(xb @ w_gu[e] + b_gu[e], 2, axis=-1)
        gate = jnp.minimum(gate, SWIGLU_LIMIT)
        up = jnp.clip(up, -SWIGLU_LIMIT, SWIGLU_LIMIT)
        act = (up + 1) * (gate * jax.nn.sigmoid(SWIGLU_ALPHA * gate))
        return act @ w_dn[e] + b_dn[e]

    yb = lax.map(expert_block, (slot_tok.reshape(n_blocks, MOE_BLOCK), block_e))
    y = jnp.zeros((T + 1, D), jnp.float32).at[slot_tok].add(
        yb.reshape(n_slots, D).astype(jnp.float32) * slot_w[:, None])
    return y[:T].reshape(B, S, D).astype(h.dtype)


def _normal(k, shape, scale):
    return jax.random.normal(k, shape, jnp.float32) * scale


def setup_inputs(seed: int = 0) -> dict:
    key = jax.random.key(seed)
    ks = jax.random.split(key, 32)
    L, D = DEPTH, D_MODEL
    return {
        "x": _normal(ks[0], (BATCH, SEQ, D), 1.0),
        "c": _normal(ks[1], (BATCH, D), 1.0),
        "w_mod": _normal(ks[2], (L, D, 6 * D), 0.5 * D ** -0.5),
        "b_mod": _normal(ks[3], (L, 6 * D), 0.02),
        "w_in": _normal(ks[4], (L, D, D_IN), D ** -0.5),
        "b_in": _normal(ks[5], (L, D_IN), 0.02),
        "conv_w": _normal(ks[6], (L, CONV_WIDTH, D_CONV), CONV_WIDTH ** -0.5),
        "conv_b": _normal(ks[7], (L, D_CONV), 0.02),
        "ln_a_g": 1.0 + _normal(ks[8], (L, D_CONV), 0.05),
        "ln_a_b": _normal(ks[9], (L, D_CONV), 0.02),
        "w_a": _normal(ks[10], (L, D_CONV, D), D_CONV ** -0.5),
        "b_a": _normal(ks[11], (L, D), 0.02),
        "ln_v_g": 1.0 + _normal(ks[12], (L, D_SGU), 0.05),
        "ln_v_b": _normal(ks[13], (L, D_SGU), 0.02),
        "w_s": _normal(ks[14], (L, SGU_GROUPS, SGU_BLOCK, SGU_BLOCK), SGU_BLOCK ** -0.5),
        "b_s": 1.0 + _normal(ks[15], (L, SGU_GROUPS, SGU_BLOCK), 0.1),
        "w_b": _normal(ks[16], (L, D_SGU, D), D_SGU ** -0.5),
        "b_b": _normal(ks[17], (L, D), 0.02),
        "w_out": _normal(ks[18], (L, D, D), DEEPNORM_BETA * D ** -0.5),
        "b_out": _normal(ks[19], (L, D), 0.02),
        "post1_g": 1.0 + _normal(ks[20], (L, D), 0.05),
        "post1_b": _normal(ks[21], (L, D), 0.02),
        "w_router": _normal(ks[22], (L, D, N_EXPERTS), D ** -0.5),
        "b_router": _normal(ks[23], (L, N_EXPERTS), 0.01),
        "w_gu": _normal(ks[24], (L, N_EXPERTS, D, 2 * D_EXPERT), D ** -0.5),
        "b_gu": _normal(ks[25], (L, N_EXPERTS, 2 * D_EXPERT), 0.02),
        "w_dn": _normal(ks[26], (L, N_EXPERTS, D_EXPERT, D), DEEPNORM_BETA * D_EXPERT ** -0.5),
        "b_dn": _normal(ks[27], (L, N_EXPERTS, D), 0.02),
        "post2_g": 1.0 + _normal(ks[28], (L, D), 0.05),
        "post2_b": _normal(ks[29], (L, D), 0.02),
    }


def reference(x, c, w_mod, b_mod, w_in, b_in, conv_w, conv_b, ln_a_g, ln_a_b, w_a, b_a,
              ln_v_g, ln_v_b, w_s, b_s, w_b, b_b, w_out, b_out, post1_g, post1_b,
              w_router, b_router, w_gu, b_gu, w_dn, b_dn, post2_g, post2_b):
    cond = jax.nn.silu(c)
    for l in range(DEPTH):
        mod = cond @ w_mod[l] + b_mod[l]
        sh1, sc1, g1, sh2, sc2, g2 = [m[:, None, :] for m in jnp.split(mod, 6, axis=-1)]
        h = _layer_norm(x) * (1 + sc1) + sh1
        y = _token_mixer(h, w_in[l], b_in[l], conv_w[l], conv_b[l], ln_a_g[l], ln_a_b[l],
                         w_a[l], b_a[l], ln_v_g[l], ln_v_b[l], w_s[l], b_s[l], w_b[l], b_b[l],
                         w_out[l], b_out[l])
        x = _layer_norm(DEEPNORM_ALPHA * x + g1 * y, post1_g[l], post1_b[l])
        h = _layer_norm(x) * (1 + sc2) + sh2
        y = _moe(h, w_router[l], b_router[l], w_gu[l], b_gu[l], w_dn[l], b_dn[l])
        x = _layer_norm(DEEPNORM_ALPHA * x + g2 * y, post2_g[l], post2_b[l])
    return x
```

```python
import functools

import jax
import jax.numpy as jnp
from jax import lax
from jax.experimental import pallas as pl
from jax.experimental.pallas import tpu as pltpu

F32 = jnp.float32
BF16 = jnp.bfloat16

LN_EPS = 1e-5
TOP_K = 4
CHUNK = 64
SWIGLU_LIMIT = 7.0
SWIGLU_ALPHA = 1.702
LANES = 128
SUBLANES = 8
VMEM_LIMIT = 56 * 1024 * 1024

MIX_ROWS = 256
CONV_ROWS = 64
MOE_ROWS = 256
POST_ROWS = 512


def _ln(x):
    mu = jnp.mean(x, axis=-1, keepdims=True)
    xc = x - mu
    var = jnp.mean(xc * xc, axis=-1, keepdims=True)
    return xc * lax.rsqrt(var + LN_EPS)


def _bdot(a, b):
    return jnp.dot(a, b, preferred_element_type=F32)


def _mod_kernel(c_ref, w_ref, b_ref, o_ref):
    c = c_ref[...]
    cond = c * jax.nn.sigmoid(c)
    o_ref[...] = jnp.dot(cond, w_ref[...], preferred_element_type=F32,
                         precision=lax.Precision.HIGHEST) + b_ref[...]


def _modulation(c, w_mod, b_mod):
    n_layers, d, d6 = w_mod.shape
    bsz = c.shape[0]
    n_chunks = d6 // d
    out = pl.pallas_call(
        _mod_kernel,
        grid=(n_layers, n_chunks),
        in_specs=[
            pl.BlockSpec((bsz, d), lambda l, j: (0, 0)),
            pl.BlockSpec((None, d, d), lambda l, j: (l, 0, j)),
            pl.BlockSpec((None, 1, d), lambda l, j: (l, 0, j)),
        ],
        out_specs=pl.BlockSpec((None, bsz, d), lambda l, j: (l, 0, j)),
        out_shape=jax.ShapeDtypeStruct((n_layers, bsz, d6), F32),
        compiler_params=pltpu.CompilerParams(
            dimension_semantics=("arbitrary", "arbitrary"), vmem_limit_bytes=VMEM_LIMIT),
        name="modulation",
    )(c, w_mod, b_mod.reshape(n_layers, 1, d6))
    return out.reshape(n_layers, bsz, n_chunks, d)


def _mixer_kernel(x_ref, mod_ref, w_in_ref, b_in_ref, cw_ref, cb_ref, lnag_ref, lnab_ref,
                  w_a_ref, b_a_ref, lnvg_ref, lnvb_ref, w_s_ref, bs_ref, w_b_ref, b_b_ref,
                  w_out_ref, b_out_ref, p1g_ref, p1b_ref, w_r_ref, b_r_ref,
                  x1_ref, h2_ref, logit_ref,
                  cbuf_ref, cv_ref, *, alpha, halo):
    rows, d = x_ref.shape
    n_col = d // LANES
    kw = cw_ref.shape[1]
    n_groups, blk, _ = w_s_ref.shape
    gdim = d // n_groups
    i = pl.program_id(1)

    x = x_ref[...]
    sh1, sc1, g1 = mod_ref[0:1, :], mod_ref[1:2, :], mod_ref[2:3, :]
    sh2, sc2 = mod_ref[3:4, :], mod_ref[4:5, :]
    hb = (_ln(x) * (1.0 + sc1) + sh1).astype(BF16)

    def proj(lo, hi):
        return _bdot(hb, w_in_ref[:, lo:hi]) + b_in_ref[:, lo:hi]

    glu = proj(0, d) * jax.nn.sigmoid(proj(d, 2 * d))

    @pl.when(i == 0)
    def _():
        cbuf_ref[:, 0:halo, :] = jnp.zeros((n_col, halo, LANES), F32)

    for c in range(n_col):
        cbuf_ref[c, halo:halo + rows, :] = glu[:, c * LANES:(c + 1) * LANES]

    def conv_cols(c, carry):
        for r0 in range(0, rows, CONV_ROWS):
            acc = jnp.broadcast_to(cb_ref[c], (CONV_ROWS, LANES))
            for j in range(kw):
                off = halo - (kw - 1) + j + r0
                acc = acc + cw_ref[c, j:j + 1, :] * cbuf_ref[c, pl.ds(off, CONV_ROWS), :]
            cv_ref[c, r0:r0 + CONV_ROWS, :] = acc
        cbuf_ref[c, 0:halo, :] = cbuf_ref[c, rows:rows + halo, :]
        return carry

    lax.fori_loop(0, n_col, conv_cols, 0)
    cv = jnp.concatenate([cv_ref[c] for c in range(n_col)], axis=1)
    ya = _ln(cv) * lnag_ref[...] + lnab_ref[...]
    ya = ya * jax.nn.sigmoid(ya)
    ya = _bdot(ya.astype(BF16), w_a_ref[...]) + b_a_ref[...]

    sg = proj(2 * d, 4 * d)
    ge = 0.5 * sg * (1.0 + lax.erf(sg * (0.5 ** 0.5)))
    u = ge[:, 0:d]
    vn = (_ln(ge[:, d:2 * d]) * lnvg_ref[...] + lnvb_ref[...]).astype(BF16)
    ci = lax.broadcasted_iota(jnp.int32, (blk, blk), 0) // CHUNK
    cj = lax.broadcasted_iota(jnp.int32, (blk, blk), 1) // CHUNK
    mask = ci >= cj
    wm = [jnp.where(mask, w_s_ref[g], 0.0).astype(BF16) for g in range(n_groups)]
    mixed_rows = []
    for r0 in range(0, rows, blk):
        parts = [_bdot(wm[g], vn[r0:r0 + blk, g * gdim:(g + 1) * gdim]) for g in range(n_groups)]
        mixed_rows.append(jnp.concatenate(parts, axis=1) + bs_ref[...])
    mixed = jnp.concatenate(mixed_rows, axis=0)
    yb = _bdot((u * mixed).astype(BF16), w_b_ref[...]) + b_b_ref[...]

    m = jax.nn.sigmoid(proj(4 * d, 5 * d)) * ya + jax.nn.sigmoid(proj(5 * d, 6 * d)) * yb
    y = _bdot(m.astype(BF16), w_out_ref[...]) + b_out_ref[...]
    x1 = _ln(alpha * x + g1 * y) * p1g_ref[...] + p1b_ref[...]
    x1_ref[...] = x1

    h2 = _ln(x1) * (1.0 + sc2) + sh2
    h2_ref[...] = h2
    logit_ref[...] = jnp.dot(h2, w_r_ref[...], preferred_element_type=F32,
                             precision=lax.Precision.HIGHEST) + b_r_ref[...]


def _const_spec(shape):
    zeros = (0,) * len(shape)
    return pl.BlockSpec(shape, lambda b, i: zeros, pipeline_mode=pl.Buffered(1))


def _mixer(x, mod_l, lw, alpha):
    bsz, seq, d = x.shape
    rows = MIX_ROWS
    kw = lw["cw"].shape[1]
    halo = -(-(kw - 1) // SUBLANES) * SUBLANES
    n_col = d // LANES
    n_exp = lw["w_r"].shape[1]
    assert seq % rows == 0 and rows % lw["w_s"].shape[1] == 0 and d % LANES == 0

    row_spec = pl.BlockSpec((None, rows, d), lambda b, i: (b, i, 0))
    consts = [lw[k] for k in ("w_in", "b_in", "cw", "cb", "ln_a_g", "ln_a_b", "w_a", "b_a",
                              "ln_v_g", "ln_v_b", "w_s", "bs", "w_b", "b_b", "w_out", "b_out",
                              "post1_g", "post1_b", "w_r", "b_r")]
    in_specs = [row_spec, pl.BlockSpec((None,) + mod_l.shape[1:], lambda b, i: (b, 0, 0))]
    in_specs += [_const_spec(a.shape) for a in consts]
    return pl.pallas_call(
        functools.partial(_mixer_kernel, alpha=alpha, halo=halo),
        grid=(bsz, seq // rows),
        in_specs=in_specs,
        out_specs=[row_spec, row_spec,
                   pl.BlockSpec((None, rows, n_exp), lambda b, i: (b, i, 0))],
        out_shape=[jax.ShapeDtypeStruct((bsz, seq, d), F32),
                   jax.ShapeDtypeStruct((bsz, seq, d), F32),
                   jax.ShapeDtypeStruct((bsz, seq, n_exp), F32)],
        scratch_shapes=[pltpu.VMEM((n_col, halo + rows, LANES), F32),
                        pltpu.VMEM((n_col, rows, LANES), F32)],
        compiler_params=pltpu.CompilerParams(
            dimension_semantics=("arbitrary", "arbitrary"), vmem_limit_bytes=VMEM_LIMIT),
        name="token_mixer",
    )(x, mod_l, *consts)


def _moe_kernel(blk_e_ref, nv_ref,
                tok_ref, tokn_ref, dst_ref, dstp_ref,
                h_hbm, w_gu_ref, b_gu_ref, w_dn_ref, b_dn_ref,
                y_hbm,
                xbuf, ybuf, wgu_bf, wdn_bf, gsem, ssem):
    i = pl.program_id(0)
    n_blocks = pl.num_programs(0)
    d_exp = wdn_bf.shape[0]
    slot = lax.rem(i, 2)
    nv = nv_ref[i]
    nv_next = jnp.where(i + 1 < n_blocks, nv_ref[jnp.minimum(i + 1, n_blocks - 1)], 0)
    nv_prev = jnp.where(i >= 1, nv_ref[jnp.maximum(i - 1, 0)], 0)

    def gather_copy(idx_ref, j, s):
        return pltpu.make_async_copy(h_hbm.at[pl.ds(idx_ref[0, j], 1), :],
                                     xbuf.at[s, pl.ds(j, 1), :], gsem.at[s])

    def scatter_copy(idx_ref, j):
        return pltpu.make_async_copy(ybuf.at[pl.ds(j, 1), :],
                                     y_hbm.at[pl.ds(idx_ref[0, j], 1), :], ssem.at[0])

    def for_rows(n, fn):
        def body(j, carry):
            fn(j)
            return carry
        lax.fori_loop(0, n, body, 0)

    @pl.when(i == 0)
    def _():
        xbuf[...] = jnp.zeros(xbuf.shape, xbuf.dtype)
        for_rows(nv, lambda j: gather_copy(tok_ref, j, 0).start())

    for_rows(nv_next, lambda j: gather_copy(tokn_ref, j, 1 - slot).start())
    for_rows(nv_prev, lambda j: scatter_copy(dstp_ref, j).wait())

    @pl.when(nv > 0)
    def _():
        changed = jnp.logical_or(i == 0, blk_e_ref[i] != blk_e_ref[jnp.maximum(i - 1, 0)])

        @pl.when(changed)
        def _():
            wgu_bf[...] = w_gu_ref[...].astype(BF16)
            wdn_bf[...] = w_dn_ref[...].astype(BF16)

        for_rows(nv, lambda j: gather_copy(tok_ref, j, slot).wait())

        xb = xbuf[slot].astype(BF16)
        gu = _bdot(xb, wgu_bf[...]) + b_gu_ref[...]
        gate = jnp.minimum(gu[:, 0:d_exp], SWIGLU_LIMIT)
        up = jnp.clip(gu[:, d_exp:2 * d_exp], -SWIGLU_LIMIT, SWIGLU_LIMIT)
        act = (up + 1.0) * (gate * jax.nn.sigmoid(SWIGLU_ALPHA * gate))
        ybuf[...] = _bdot(act.astype(BF16), wdn_bf[...]) + b_dn_ref[...]
        for_rows(nv, lambda j: scatter_copy(dst_ref, j).start())

        @pl.when(i == n_blocks - 1)
        def _():
            for_rows(nv, lambda j: scatter_copy(dst_ref, j).wait())


def _moe(h2, tok, dst, blk_e, n_valid, w_gu, b_gu, w_dn, b_dn, layer):
    t, d = h2.shape
    n_blocks, _, rows = tok.shape
    n_exp, _, d2 = w_gu.shape[1:]
    d_exp = d2 // 2

    def smem_spec(index_map):
        return pl.BlockSpec((None, 1, rows), index_map, memory_space=pltpu.SMEM)

    grid_spec = pltpu.PrefetchScalarGridSpec(
        num_scalar_prefetch=2,
        grid=(n_blocks,),
        in_specs=[
            smem_spec(lambda i, be, nu: (i, 0, 0)),
            smem_spec(lambda i, be, nu: (jnp.minimum(i + 1, n_blocks - 1), 0, 0)),
            smem_spec(lambda i, be, nu: (i, 0, 0)),
            smem_spec(lambda i, be, nu: (jnp.maximum(i - 1, 0), 0, 0)),
            pl.BlockSpec(memory_space=pl.ANY),
            pl.BlockSpec((None, None, d, d2), lambda i, be, nu: (layer, be[i], 0, 0)),
            pl.BlockSpec((None, None, 1, d2), lambda i, be, nu: (layer, be[i], 0, 0)),
            pl.BlockSpec((None, None, d_exp, d), lambda i, be, nu: (layer, be[i], 0, 0)),
            pl.BlockSpec((None, None, 1, d), lambda i, be, nu: (layer, be[i], 0, 0)),
        ],
        out_specs=pl.BlockSpec(memory_space=pl.ANY),
        scratch_shapes=[
            pltpu.VMEM((2, rows, d), F32),
            pltpu.VMEM((rows, d), F32),
            pltpu.VMEM((d, d2), BF16),
            pltpu.VMEM((d_exp, d), BF16),
            pltpu.SemaphoreType.DMA((2,)),
            pltpu.SemaphoreType.DMA((1,)),
        ],
    )
    n_layers = w_gu.shape[0]
    return pl.pallas_call(
        _moe_kernel,
        grid_spec=grid_spec,
        out_shape=jax.ShapeDtypeStruct((TOP_K * t, d), F32),
        compiler_params=pltpu.CompilerParams(
            dimension_semantics=("arbitrary",), vmem_limit_bytes=VMEM_LIMIT),
        name="expert_mlp",
    )(blk_e, n_valid, tok, tok, dst, dst, h2, w_gu,
      b_gu.reshape(n_layers, n_exp, 1, d2), w_dn, b_dn.reshape(n_layers, n_exp, 1, d))


def _route(logits, rows):
    t, n_exp = logits.shape
    n_assign = t * TOP_K
    n_blocks = n_assign // rows + n_exp
    n_slots = n_blocks * rows
    top_logits, top_idx = lax.top_k(logits, TOP_K)
    top_w = jax.nn.softmax(top_logits, axis=-1)
    flat_e = top_idx.reshape(-1)
    onehot = (flat_e[:, None] == jnp.arange(n_exp, dtype=flat_e.dtype)[None, :]).astype(jnp.int32)
    csum = jnp.cumsum(onehot, axis=0)
    rank = jnp.take_along_axis(csum, flat_e[:, None], axis=1)[:, 0] - 1
    counts = csum[-1]
    blocks_e = (counts + rows - 1) // rows
    blk_end = jnp.cumsum(blocks_e)
    pad_start = (blk_end - blocks_e) * rows
    dest = pad_start[flat_e] + rank
    assign = jnp.arange(n_assign, dtype=jnp.int32)
    tok = jnp.zeros((n_slots,), jnp.int32).at[dest].set(assign // TOP_K)
    dst = jnp.zeros((n_slots,), jnp.int32).at[dest].set((assign % TOP_K) * t + assign // TOP_K)
    n_used = blk_end[-1]
    bidx = jnp.arange(n_blocks, dtype=blk_end.dtype)
    blk_e = jnp.minimum(jnp.searchsorted(blk_end, jnp.minimum(bidx, n_used - 1), side="right"), n_exp - 1)
    n_valid = jnp.clip(counts[blk_e] - (bidx - (blk_end - blocks_e)[blk_e]) * rows, 0, rows)
    n_valid = jnp.where(bidx < n_used, n_valid, 0)
    return (tok.reshape(n_blocks, 1, rows), dst.reshape(n_blocks, 1, rows),
            blk_e.astype(jnp.int32), n_valid.astype(jnp.int32), top_w)


def _post_kernel(x1_ref, mod_ref, tw_ref, y0_ref, y1_ref, y2_ref, y3_ref, g_ref, b_ref, o_ref, *, alpha):
    tw = tw_ref[...]
    y = tw[:, 0:1] * y0_ref[...]
    for k, y_ref in enumerate((y1_ref, y2_ref, y3_ref), start=1):
        y = y + tw[:, k:k + 1] * y_ref[...]
    g2 = mod_ref[5:6, :]
    o_ref[...] = _ln(alpha * x1_ref[...] + g2 * y) * g_ref[...] + b_ref[...]


def _post(x1, mod_l, top_w, ytk, g, b, alpha):
    bsz, seq, d = x1.shape
    rows = POST_ROWS
    per_b = seq // rows
    per_k = bsz * per_b
    row_spec = pl.BlockSpec((None, rows, d), lambda bi, i: (bi, i, 0))

    def y_spec(k):
        return pl.BlockSpec((rows, d), lambda bi, i: (k * per_k + bi * per_b + i, 0))

    return pl.pallas_call(
        functools.partial(_post_kernel, alpha=alpha),
        grid=(bsz, per_b),
        in_specs=[row_spec,
                  pl.BlockSpec((None,) + mod_l.shape[1:], lambda bi, i: (bi, 0, 0)),
                  pl.BlockSpec((None, rows, TOP_K), lambda bi, i: (bi, i, 0)),
                  y_spec(0), y_spec(1), y_spec(2), y_spec(3),
                  pl.BlockSpec((1, d), lambda bi, i: (0, 0)),
                  pl.BlockSpec((1, d), lambda bi, i: (0, 0))],
        out_specs=row_spec,
        out_shape=jax.ShapeDtypeStruct((bsz, seq, d), F32),
        compiler_params=pltpu.CompilerParams(
            dimension_semantics=("arbitrary", "arbitrary"), vmem_limit_bytes=VMEM_LIMIT),
        name="combine_postnorm",
    )(x1, mod_l, top_w.reshape(bsz, seq, TOP_K), ytk, ytk, ytk, ytk, g, b)


def kernel(x, c, w_mod, b_mod, w_in, b_in, conv_w, conv_b, ln_a_g, ln_a_b, w_a, b_a, ln_v_g, ln_v_b, w_s, b_s, w_b, b_b, w_out, b_out, post1_g, post1_b, w_router, b_router, w_gu, b_gu, w_dn, b_dn, post2_g, post2_b):
    n_layers = w_in.shape[0]
    bsz, seq, d = x.shape
    t = bsz * seq
    n_col = d // LANES
    n_groups, blk = w_s.shape[1], w_s.shape[2]
    alpha = float((2 * n_layers) ** 0.25)

    mod = _modulation(c, w_mod, b_mod)
    w_in_bf, w_a_bf = w_in.astype(BF16), w_a.astype(BF16)
    w_b_bf, w_out_bf = w_b.astype(BF16), w_out.astype(BF16)
    kw = conv_w.shape[1]
    cw = conv_w.reshape(n_layers, kw, n_col, LANES).transpose(0, 2, 1, 3)
    cb = conv_b.reshape(n_layers, n_col, 1, LANES)
    bs = jnp.repeat(jnp.transpose(b_s, (0, 2, 1)), d // n_groups, axis=2)

    def row(a, l):
        return a[l].reshape(1, -1)

    for l in range(n_layers):
        lw = dict(w_in=w_in_bf[l], b_in=row(b_in, l), cw=cw[l], cb=cb[l],
                  ln_a_g=row(ln_a_g, l), ln_a_b=row(ln_a_b, l), w_a=w_a_bf[l], b_a=row(b_a, l),
                  ln_v_g=row(ln_v_g, l), ln_v_b=row(ln_v_b, l), w_s=w_s[l], bs=bs[l],
                  w_b=w_b_bf[l], b_b=row(b_b, l), w_out=w_out_bf[l], b_out=row(b_out, l),
                  post1_g=row(post1_g, l), post1_b=row(post1_b, l),
                  w_r=w_router[l], b_r=row(b_router, l))
        x1, h2, logits = _mixer(x, mod[l], lw, alpha)
        tok, dst, blk_e, n_valid, top_w = _route(logits.reshape(t, -1), MOE_ROWS)
        ytk = _moe(h2.reshape(t, d), tok, dst, blk_e, n_valid, w_gu, b_gu, w_dn, b_dn, l)
        x = _post(x1, mod[l], top_w, ytk, row(post2_g, l), row(post2_b, l), alpha)
    return x
```

```python
import functools

import jax
import jax.numpy as jnp
from jax import lax
from jax.experimental import pallas as pl
from jax.experimental.pallas import tpu as pltpu

F32 = jnp.float32
BF16 = jnp.bfloat16

LN_EPS = 1e-5
TOP_K = 4
CHUNK = 64
SWIGLU_LIMIT = 7.0
SWIGLU_ALPHA = 1.702
LANES = 128
SUBLANES = 8
VMEM_LIMIT = 56 * 1024 * 1024

MIX_ROWS = 256
CONV_ROWS = 64
MOE_ROWS = 256
POST_ROWS = 512


def _ln(x):
    mu = jnp.mean(x, axis=-1, keepdims=True)
    xc = x - mu
    var = jnp.mean(xc * xc, axis=-1, keepdims=True)
    return xc * lax.rsqrt(var + LN_EPS)


def _bdot(a, b):
    return jnp.dot(a, b, preferred_element_type=F32)


def _mod_kernel(c_ref, w_ref, b_ref, o_ref):
    c = c_ref[...]
    cond = c * jax.nn.sigmoid(c)
    o_ref[...] = jnp.dot(cond, w_ref[...], preferred_element_type=F32,
                         precision=lax.Precision.HIGHEST) + b_ref[...]


def _modulation(c, w_mod, b_mod):
    n_layers, d, d6 = w_mod.shape
    bsz = c.shape[0]
    n_chunks = d6 // d
    out = pl.pallas_call(
        _mod_kernel,
        grid=(n_layers, n_chunks),
        in_specs=[
            pl.BlockSpec((bsz, d), lambda l, j: (0, 0)),
            pl.BlockSpec((None, d, d), lambda l, j: (l, 0, j)),
            pl.BlockSpec((None, 1, d), lambda l, j: (l, 0, j)),
        ],
        out_specs=pl.BlockSpec((None, bsz, d), lambda l, j: (l, 0, j)),
        out_shape=jax.ShapeDtypeStruct((n_layers, bsz, d6), F32),
        compiler_params=pltpu.CompilerParams(
            dimension_semantics=("arbitrary", "arbitrary"), vmem_limit_bytes=VMEM_LIMIT),
        name="modulation",
    )(c, w_mod, b_mod.reshape(n_layers, 1, d6))
    return out.reshape(n_layers, bsz, n_chunks, d)


def _mixer_kernel(x_ref, mod_ref, w_in_ref, b_in_ref, cw_ref, cb_ref, lnag_ref, lnab_ref,
                  w_a_ref, b_a_ref, lnvg_ref, lnvb_ref, w_s_ref, bs_ref, w_b_ref, b_b_ref,
                  w_out_ref, b_out_ref, p1g_ref, p1b_ref, w_r_ref, b_r_ref,
                  x1_ref, h2_ref, logit_ref,
                  cbuf_ref, cv_ref, *, alpha, halo):
    rows, d = x_ref.shape
    n_col = d // LANES
    kw = cw_ref.shape[1]
    n_groups, blk, _ = w_s_ref.shape
    gdim = d // n_groups
    i = pl.program_id(1)

    x = x_ref[...]
    sh1, sc1, g1 = mod_ref[0:1, :], mod_ref[1:2, :], mod_ref[2:3, :]
    sh2, sc2 = mod_ref[3:4, :], mod_ref[4:5, :]
    hb = (_ln(x) * (1.0 + sc1) + sh1).astype(BF16)

    def proj(lo, hi):
        return _bdot(hb, w_in_ref[:, lo:hi]) + b_in_ref[:, lo:hi]

    glu = proj(0, d) * jax.nn.sigmoid(proj(d, 2 * d))

    @pl.when(i == 0)
    def _():
        cbuf_ref[:, 0:halo, :] = jnp.zeros((n_col, halo, LANES), F32)

    for c in range(n_col):
        cbuf_ref[c, halo:halo + rows, :] = glu[:, c * LANES:(c + 1) * LANES]

    def conv_cols(c, carry):
        for r0 in range(0, rows, CONV_ROWS):
            acc = jnp.broadcast_to(cb_ref[c], (CONV_ROWS, LANES))
            for j in range(kw):
                off = halo - (kw - 1) + j + r0
                acc = acc + cw_ref[c, j:j + 1, :] * cbuf_ref[c, pl.ds(off, CONV_ROWS), :]
            cv_ref[c, r0:r0 + CONV_ROWS, :] = acc
        cbuf_ref[c, 0:halo, :] = cbuf_ref[c, rows:rows + halo, :]
        return carry

    lax.fori_loop(0, n_col, conv_cols, 0)
    cv = jnp.concatenate([cv_ref[c] for c in range(n_col)], axis=1)
    ya = _ln(cv) * lnag_ref[...] + lnab_ref[...]
    ya = ya * jax.nn.sigmoid(ya)
    ya = _bdot(ya.astype(BF16), w_a_ref[...]) + b_a_ref[...]

    sg = proj(2 * d, 4 * d)
    ge = 0.5 * sg * (1.0 + lax.erf(sg * (0.5 ** 0.5)))
    u = ge[:, 0:d]
    vn = (_ln(ge[:, d:2 * d]) * lnvg_ref[...] + lnvb_ref[...]).astype(BF16)
    ci = lax.broadcasted_iota(jnp.int32, (blk, blk), 0) // CHUNK
    cj = lax.broadcasted_iota(jnp.int32, (blk, blk), 1) // CHUNK
    mask = ci >= cj
    wm = [jnp.where(mask, w_s_ref[g], 0.0).astype(BF16) for g in range(n_groups)]
    mixed_rows = []
    for r0 in range(0, rows, blk):
        parts = [_bdot(wm[g], vn[r0:r0 + blk, g * gdim:(g + 1) * gdim]) for g in range(n_groups)]
        mixed_rows.append(jnp.concatenate(parts, axis=1) + bs_ref[...])
    mixed = jnp.concatenate(mixed_rows, axis=0)
    yb = _bdot((u * mixed).astype(BF16), w_b_ref[...]) + b_b_ref[...]

    m = jax.nn.sigmoid(proj(4 * d, 5 * d)) * ya + jax.nn.sigmoid(proj(5 * d, 6 * d)) * yb
    y = _bdot(m.astype(BF16), w_out_ref[...]) + b_out_ref[...]
    x1 = _ln(alpha * x + g1 * y) * p1g_ref[...] + p1b_ref[...]
    x1_ref[...] = x1

    h2 = _ln(x1) * (1.0 + sc2) + sh2
    h2_ref[...] = h2
    logit_ref[...] = jnp.dot(h2, w_r_ref[...], preferred_element_type=F32,
                             precision=lax.Precision.HIGHEST) + b_r_ref[...]


def _const_spec(shape):
    zeros = (0,) * len(shape)
    return pl.BlockSpec(shape, lambda b, i: zeros, pipeline_mode=pl.Buffered(1))


def _mixer(x, mod_l, lw, alpha):
    bsz, seq, d = x.shape
    rows = MIX_ROWS
    kw = lw["cw"].shape[1]
    halo = -(-(kw - 1) // SUBLANES) * SUBLANES
    n_col = d // LANES
    n_exp = lw["w_r"].shape[1]
    assert seq % rows == 0 and rows % lw["w_s"].shape[1] == 0 and d % LANES == 0

    row_spec = pl.BlockSpec((None, rows, d), lambda b, i: (b, i, 0))
    consts = [lw[k] for k in ("w_in", "b_in", "cw", "cb", "ln_a_g", "ln_a_b", "w_a", "b_a",
                              "ln_v_g", "ln_v_b", "w_s", "bs", "w_b", "b_b", "w_out", "b_out",
                              "post1_g", "post1_b", "w_r", "b_r")]
    in_specs = [row_spec, pl.BlockSpec((None,) + mod_l.shape[1:], lambda b, i: (b, 0, 0))]
    in_specs += [_const_spec(a.shape) for a in consts]
    return pl.pallas_call(
        functools.partial(_mixer_kernel, alpha=alpha, halo=halo),
        grid=(bsz, seq // rows),
        in_specs=in_specs,
        out_specs=[row_spec, row_spec,
                   pl.BlockSpec((None, rows, n_exp), lambda b, i: (b, i, 0))],
        out_shape=[jax.ShapeDtypeStruct((bsz, seq, d), F32),
                   jax.ShapeDtypeStruct((bsz, seq, d), F32),
                   jax.ShapeDtypeStruct((bsz, seq, n_exp), F32)],
        scratch_shapes=[pltpu.VMEM((n_col, halo + rows, LANES), F32),
                        pltpu.VMEM((n_col, rows, LANES), F32)],
        compiler_params=pltpu.CompilerParams(
            dimension_semantics=("arbitrary", "arbitrary"), vmem_limit_bytes=VMEM_LIMIT),
        name="token_mixer",
    )(x, mod_l, *consts)


def _moe_kernel(blk_e_ref, nv_ref,
                tok_ref, tokn_ref, dst_ref, dstp_ref,
                h_hbm, w_gu_ref, b_gu_ref, w_dn_ref, b_dn_ref,
                y_hbm,
                xbuf, ybuf, wgu_bf, wdn_bf, gsem, ssem):
    i = pl.program_id(0)
    n_blocks = pl.num_programs(0)
    d_exp = wdn_bf.shape[0]
    slot = lax.rem(i, 2)
    nv = nv_ref[i]
    nv_next = jnp.where(i + 1 < n_blocks, nv_ref[jnp.minimum(i + 1, n_blocks - 1)], 0)
    nv_prev = jnp.where(i >= 1, nv_ref[jnp.maximum(i - 1, 0)], 0)

    rows = ybuf.shape[0]
    group = SUBLANES
    sizes = [1 << b for b in range(rows.bit_length() - 1, -1, -1)]

    def gather_copy(idx_ref, j, s):
        return pltpu.make_async_copy(h_hbm.at[pl.ds(idx_ref[0, j], 1), :],
                                     xbuf.at[s, pl.ds(j, 1), :], gsem.at[s])

    def scatter_copy(idx_ref, j):
        return pltpu.make_async_copy(ybuf.at[pl.ds(j, 1), :],
                                     y_hbm.at[pl.ds(idx_ref[0, j], 1), :], ssem.at[0])

    def start_gathers(idx_ref, n, s):
        def body(g, carry):
            for u in range(group):
                gather_copy(idx_ref, g * group + u, s).start()
            return carry
        lax.fori_loop(0, (n + group - 1) // group, body, 0)

    def wait_gathers(n, s):
        n_started = (n + group - 1) // group * group
        for p in sizes:
            if p >= group:
                @pl.when((n_started & p) != 0)
                def _():
                    pltpu.make_async_copy(h_hbm.at[pl.ds(0, p), :], xbuf.at[s, pl.ds(0, p), :],
                                          gsem.at[s]).wait()

    def start_scatters(idx_ref, n):
        def body(g, carry):
            for u in range(group):
                scatter_copy(idx_ref, g * group + u).start()
            return carry
        lax.fori_loop(0, n // group, body, 0)
        base = n // group * group
        for p in sizes:
            if p < group:
                @pl.when((n & p) != 0)
                def _():
                    for u in range(p):
                        scatter_copy(idx_ref, base + (n & (group - 2 * p)) + u).start()

    def wait_scatters(n):
        for p in sizes:
            @pl.when((n & p) != 0)
            def _():
                pltpu.make_async_copy(ybuf.at[pl.ds(0, p), :], y_hbm.at[pl.ds(0, p), :],
                                      ssem.at[0]).wait()

    @pl.when(i == 0)
    def _():
        xbuf[...] = jnp.zeros(xbuf.shape, xbuf.dtype)
        start_gathers(tok_ref, nv, 0)

    start_gathers(tokn_ref, nv_next, 1 - slot)
    wait_scatters(nv_prev)

    @pl.when(nv > 0)
    def _():
        changed = jnp.logical_or(i == 0, blk_e_ref[i] != blk_e_ref[jnp.maximum(i - 1, 0)])

        @pl.when(changed)
        def _():
            wgu_bf[...] = w_gu_ref[...].astype(BF16)
            wdn_bf[...] = w_dn_ref[...].astype(BF16)

        wait_gathers(nv, slot)

        xb = xbuf[slot].astype(BF16)
        gu = _bdot(xb, wgu_bf[...]) + b_gu_ref[...]
        gate = jnp.minimum(gu[:, 0:d_exp], SWIGLU_LIMIT)
        up = jnp.clip(gu[:, d_exp:2 * d_exp], -SWIGLU_LIMIT, SWIGLU_LIMIT)
        act = (up + 1.0) * (gate * jax.nn.sigmoid(SWIGLU_ALPHA * gate))
        ybuf[...] = _bdot(act.astype(BF16), wdn_bf[...]) + b_dn_ref[...]
        start_scatters(dst_ref, nv)

        @pl.when(i == n_blocks - 1)
        def _():
            wait_scatters(nv)


def _moe(h2, tok, dst, blk_e, n_valid, w_gu, b_gu, w_dn, b_dn, layer):
    t, d = h2.shape
    n_blocks, _, rows = tok.shape
    n_exp, _, d2 = w_gu.shape[1:]
    d_exp = d2 // 2

    def smem_spec(index_map):
        return pl.BlockSpec((None, 1, rows), index_map, memory_space=pltpu.SMEM)

    grid_spec = pltpu.PrefetchScalarGridSpec(
        num_scalar_prefetch=2,
        grid=(n_blocks,),
        in_specs=[
            smem_spec(lambda i, be, nu: (i, 0, 0)),
            smem_spec(lambda i, be, nu: (jnp.minimum(i + 1, n_blocks - 1), 0, 0)),
            smem_spec(lambda i, be, nu: (i, 0, 0)),
            smem_spec(lambda i, be, nu: (jnp.maximum(i - 1, 0), 0, 0)),
            pl.BlockSpec(memory_space=pl.ANY),
            pl.BlockSpec((None, None, d, d2), lambda i, be, nu: (layer, be[i], 0, 0)),
            pl.BlockSpec((None, None, 1, d2), lambda i, be, nu: (layer, be[i], 0, 0)),
            pl.BlockSpec((None, None, d_exp, d), lambda i, be, nu: (layer, be[i], 0, 0)),
            pl.BlockSpec((None, None, 1, d), lambda i, be, nu: (layer, be[i], 0, 0)),
        ],
        out_specs=pl.BlockSpec(memory_space=pl.ANY),
        scratch_shapes=[
            pltpu.VMEM((2, rows, d), F32),
            pltpu.VMEM((rows, d), F32),
            pltpu.VMEM((d, d2), BF16),
            pltpu.VMEM((d_exp, d), BF16),
            pltpu.SemaphoreType.DMA((2,)),
            pltpu.SemaphoreType.DMA((1,)),
        ],
    )
    n_layers = w_gu.shape[0]
    return pl.pallas_call(
        _moe_kernel,
        grid_spec=grid_spec,
        out_shape=jax.ShapeDtypeStruct((TOP_K * t, d), F32),
        compiler_params=pltpu.CompilerParams(
            dimension_semantics=("arbitrary",), vmem_limit_bytes=VMEM_LIMIT,
            disable_bounds_checks=True),
        name="expert_mlp",
    )(blk_e, n_valid, tok, tok, dst, dst, h2, w_gu,
      b_gu.reshape(n_layers, n_exp, 1, d2), w_dn, b_dn.reshape(n_layers, n_exp, 1, d))


def _route(logits, rows):
    t, n_exp = logits.shape
    n_assign = t * TOP_K
    n_blocks = n_assign // rows + n_exp
    top_logits, top_idx = lax.top_k(logits, TOP_K)
    top_w = jax.nn.softmax(top_logits, axis=-1)
    flat_e = top_idx.reshape(-1).astype(jnp.int32)
    sorted_e, order = lax.sort((flat_e, jnp.arange(n_assign, dtype=jnp.int32)), num_keys=1, is_stable=True)
    experts = jnp.arange(n_exp, dtype=jnp.int32)
    start = jnp.searchsorted(sorted_e, experts, side="left").astype(jnp.int32)
    counts = jnp.searchsorted(sorted_e, experts, side="right").astype(jnp.int32) - start
    blocks_e = (counts + rows - 1) // rows
    blk_end = jnp.cumsum(blocks_e)
    n_used = blk_end[-1]
    bidx = jnp.arange(n_blocks, dtype=jnp.int32)
    blk_e = jnp.minimum(jnp.searchsorted(blk_end, jnp.minimum(bidx, n_used - 1), side="right"),
                        n_exp - 1).astype(jnp.int32)
    within = bidx - (blk_end - blocks_e)[blk_e]
    n_valid = jnp.where(bidx < n_used, jnp.clip(counts[blk_e] - within * rows, 0, rows), 0)
    lane = jnp.arange(rows, dtype=jnp.int32)[None, :]
    valid = lane < n_valid[:, None]
    src = jnp.clip((start[blk_e] + within * rows)[:, None] + lane, 0, n_assign - 1)
    assign = order[src]
    tok = jnp.where(valid, assign // TOP_K, 0)
    dst = jnp.where(valid, (assign % TOP_K) * t + assign // TOP_K, 0)
    return (tok.reshape(n_blocks, 1, rows), dst.reshape(n_blocks, 1, rows),
            blk_e, n_valid.astype(jnp.int32), top_w)


def _post_kernel(x1_ref, mod_ref, tw_ref, y0_ref, y1_ref, y2_ref, y3_ref, g_ref, b_ref, o_ref, *, alpha):
    tw = tw_ref[...]
    y = tw[:, 0:1] * y0_ref[...]
    for k, y_ref in enumerate((y1_ref, y2_ref, y3_ref), start=1):
        y = y + tw[:, k:k + 1] * y_ref[...]
    g2 = mod_ref[5:6, :]
    o_ref[...] = _ln(alpha * x1_ref[...] + g2 * y) * g_ref[...] + b_ref[...]


def _post(x1, mod_l, top_w, ytk, g, b, alpha):
    bsz, seq, d = x1.shape
    rows = POST_ROWS
    per_b = seq // rows
    per_k = bsz * per_b
    row_spec = pl.BlockSpec((None, rows, d), lambda bi, i: (bi, i, 0))

    def y_spec(k):
        return pl.BlockSpec((rows, d), lambda bi, i: (k * per_k + bi * per_b + i, 0))

    return pl.pallas_call(
        functools.partial(_post_kernel, alpha=alpha),
        grid=(bsz, per_b),
        in_specs=[row_spec,
                  pl.BlockSpec((None,) + mod_l.shape[1:], lambda bi, i: (bi, 0, 0)),
                  pl.BlockSpec((None, rows, TOP_K), lambda bi, i: (bi, i, 0)),
                  y_spec(0), y_spec(1), y_spec(2), y_spec(3),
                  pl.BlockSpec((1, d), lambda bi, i: (0, 0)),
                  pl.BlockSpec((1, d), lambda bi, i: (0, 0))],
        out_specs=row_spec,
        out_shape=jax.ShapeDtypeStruct((bsz, seq, d), F32),
        compiler_params=pltpu.CompilerParams(
            dimension_semantics=("arbitrary", "arbitrary"), vmem_limit_bytes=VMEM_LIMIT),
        name="combine_postnorm",
    )(x1, mod_l, top_w.reshape(bsz, seq, TOP_K), ytk, ytk, ytk, ytk, g, b)


def kernel(x, c, w_mod, b_mod, w_in, b_in, conv_w, conv_b, ln_a_g, ln_a_b, w_a, b_a, ln_v_g, ln_v_b, w_s, b_s, w_b, b_b, w_out, b_out, post1_g, post1_b, w_router, b_router, w_gu, b_gu, w_dn, b_dn, post2_g, post2_b):
    n_layers = w_in.shape[0]
    bsz, seq, d = x.shape
    t = bsz * seq
    n_col = d // LANES
    n_groups, blk = w_s.shape[1], w_s.shape[2]
    alpha = float((2 * n_layers) ** 0.25)

    mod = _modulation(c, w_mod, b_mod)
    w_in_bf, w_a_bf = w_in.astype(BF16), w_a.astype(BF16)
    w_b_bf, w_out_bf = w_b.astype(BF16), w_out.astype(BF16)
    kw = conv_w.shape[1]
    cw = conv_w.reshape(n_layers, kw, n_col, LANES).transpose(0, 2, 1, 3)
    cb = conv_b.reshape(n_layers, n_col, 1, LANES)
    bs = jnp.repeat(jnp.transpose(b_s, (0, 2, 1)), d // n_groups, axis=2)

    def row(a, l):
        return a[l].reshape(1, -1)

    for l in range(n_layers):
        lw = dict(w_in=w_in_bf[l], b_in=row(b_in, l), cw=cw[l], cb=cb[l],
                  ln_a_g=row(ln_a_g, l), ln_a_b=row(ln_a_b, l), w_a=w_a_bf[l], b_a=row(b_a, l),
                  ln_v_g=row(ln_v_g, l), ln_v_b=row(ln_v_b, l), w_s=w_s[l], bs=bs[l],
                  w_b=w_b_bf[l], b_b=row(b_b, l), w_out=w_out_bf[l], b_out=row(b_out, l),
                  post1_g=row(post1_g, l), post1_b=row(post1_b, l),
                  w_r=w_router[l], b_r=row(b_router, l))
        x1, h2, logits = _mixer(x, mod[l], lw, alpha)
        tok, dst, blk_e, n_valid, top_w = _route(logits.reshape(t, -1), MOE_ROWS)
        ytk = _moe(h2.reshape(t, d), tok, dst, blk_e, n_valid, w_gu, b_gu, w_dn, b_dn, l)
        x = _post(x1, mod[l], top_w, ytk, row(post2_g, l), row(post2_b, l), alpha)
    return x
```
